```python
import math
import jax
import jax.numpy as jnp
from jax import lax
import numpy as np

D_MODEL = 1024
BATCH = 8
SEQ = 8192
DEPTH = 1
DEC_BATCH = 128
DEC_SEQ = 1
PAST_LEN = 8192
PAGE_SIZE = 128

MIX_WIDTH = D_MODEL
ATT_WIDTH = MIX_WIDTH // 2
SSM_WIDTH = MIX_WIDTH - ATT_WIDTH
HEAD_DIM = 64
N_HEADS = ATT_WIDTH // HEAD_DIM
SSM_GROUP = 16
SSM_GROUPS = SSM_WIDTH // SSM_GROUP
SSM_STATE = 64
MOBA_BLOCK = 256
MOBA_TOPK = 3
Q_CHUNK = 32
D_FF = 128 * ((8 * D_MODEL // 3 + 127) // 128)
ROPE_THETA = 10000.0
RMS_EPS = 1e-6
DT_MIN = 0.001
DT_MAX = 0.1
PROJ_WIDTH = 3 * ATT_WIDTH + SSM_WIDTH

kernel_name = 'hymba_moba_s5_macaron_decode_step'


def rms_norm(x, g):
    xf = x.astype(jnp.float32)
    y = xf * lax.rsqrt(jnp.mean(xf * xf, axis=-1, keepdims=True) + RMS_EPS)
    return (y * g.astype(jnp.float32)).astype(x.dtype)


def swiglu(x, wg, wu, wd):
    return (jax.nn.silu(x @ wg) * (x @ wu)) @ wd


def half_step_ffn(h, g_pre, g_post, wg, wu, wd):
    return h + 0.5 * rms_norm(swiglu(rms_norm(h, g_pre), wg, wu, wd), g_post)


def rope(x, pos):
    half = HEAD_DIM // 2
    inv_freq = ROPE_THETA ** (-jnp.arange(half, dtype=jnp.float32) / half)
    ang = pos.astype(jnp.float32)[:, None] * inv_freq[None, :]
    cos = jnp.cos(ang)[None, :, None, :]
    sin = jnp.sin(ang)[None, :, None, :]
    xf = x.astype(jnp.float32)
    x1, x2 = xf[..., :half], xf[..., half:]
    return jnp.concatenate([x1 * cos - x2 * sin, x1 * sin + x2 * cos], axis=-1).astype(x.dtype)


def project(u, w_in, pos):
    n, l, _ = u.shape
    z = u @ w_in
    q, k, v, s = jnp.split(z, [ATT_WIDTH, 2 * ATT_WIDTH, 3 * ATT_WIDTH], axis=-1)
    q = q.reshape(n, l, N_HEADS, HEAD_DIM)
    k = k.reshape(n, l, N_HEADS, HEAD_DIM)
    v = v.reshape(n, l, N_HEADS, HEAD_DIM)
    return rope(q, pos), rope(k, pos), v, s


def moba_attend(q, k_own, v_own, own_mask, k_sel=None, v_sel=None, sel_valid=None):
    qf = q.astype(jnp.float32) * (HEAD_DIM ** -0.5)
    s_own = jnp.einsum('nqhd,nhmd->nqhm', qf, k_own.astype(jnp.float32))
    s_own = jnp.where(own_mask[None, :, None, :], s_own, -jnp.inf)
    if k_sel is None:
        p = jax.nn.softmax(s_own, axis=-1)
        out = jnp.einsum('nqhm,nhmd->nqhd', p, v_own.astype(jnp.float32))
        return out.astype(q.dtype)
    s_sel = jnp.einsum('nqhd,nqhtjd->nqhtj', qf, k_sel.astype(jnp.float32))
    if sel_valid is not None:
        s_sel = jnp.where(sel_valid[..., None], s_sel, -jnp.inf)
    n, nq, h, t, j = s_sel.shape
    s = jnp.concatenate([s_sel.reshape(n, nq, h, t * j), s_own], axis=-1)
    p = jax.nn.softmax(s, axis=-1)
    p_sel = p[..., :t * j].reshape(n, nq, h, t, j)
    p_own = p[..., t * j:]
    out = (jnp.einsum('nqhtj,nqhtjd->nqhd', p_sel, v_sel.astype(jnp.float32))
           + jnp.einsum('nqhm,nhmd->nqhd', p_own, v_own.astype(jnp.float32)))
    return out.astype(q.dtype)


def moba_prompt(q, k, v):
    b, l = q.shape[:2]
    n_blk = -(-l // MOBA_BLOCK)
    pad = n_blk * MOBA_BLOCK - l
    def blocks(t):
        t = jnp.pad(t, ((0, 0), (0, pad), (0, 0), (0, 0)))
        return t.reshape(b, n_blk, MOBA_BLOCK, N_HEADS, HEAD_DIM).transpose(0, 3, 1, 2, 4)
    kb = blocks(k)
    vb = blocks(v)
    n_cand = (l - 1) // MOBA_BLOCK
    topk = min(MOBA_TOPK, n_cand)
    k_mean = jnp.mean(kb[:, :, :n_cand].astype(jnp.float32), axis=3) if topk > 0 else None
    n_chunks = l // Q_CHUNK
    q_chunks = q.reshape(b, n_chunks, Q_CHUNK, N_HEADS, HEAD_DIM).transpose(1, 0, 2, 3, 4)
    b_idx = jnp.arange(b)[:, None, None, None]
    h_idx = jnp.arange(N_HEADS)[None, None, :, None]

    def one_chunk(args):
        c, qc = args
        start = c * Q_CHUNK
        qblk = start // MOBA_BLOCK
        q_pos = start + jnp.arange(Q_CHUNK)
        k_pos = qblk * MOBA_BLOCK + jnp.arange(MOBA_BLOCK)
        k_own = lax.dynamic_index_in_dim(kb, qblk, axis=2, keepdims=False)
        v_own = lax.dynamic_index_in_dim(vb, qblk, axis=2, keepdims=False)
        own_mask = k_pos[None, :] <= q_pos[:, None]
        if topk == 0:
            return moba_attend(qc, k_own, v_own, own_mask)
        scores = jnp.einsum('bqhd,bhnd->bqhn', qc.astype(jnp.float32), k_mean)
        scores = jnp.where(jnp.arange(n_cand) < qblk, scores, -jnp.inf)
        _, sel = lax.top_k(scores, topk)
        valid = sel < qblk
        k_sel = kb[b_idx, h_idx, sel]
        v_sel = vb[b_idx, h_idx, sel]
        return moba_attend(qc, k_own, v_own, own_mask, k_sel, v_sel, valid)

    out = lax.map(one_chunk, (jnp.arange(n_chunks), q_chunks))
    return out.transpose(1, 0, 2, 3, 4).reshape(b, l, ATT_WIDTH)


def moba_sample(q, k_new, v_new, cache_k, cache_v, page_table):
    nb, ds = q.shape[:2]
    ppb = MOBA_BLOCK // PAGE_SIZE
    n_full = PAST_LEN // MOBA_BLOCK
    r_pages = (PAST_LEN - n_full * MOBA_BLOCK) // PAGE_SIZE
    topk = min(MOBA_TOPK, n_full)
    own_pages = page_table[:, n_full * ppb:n_full * ppb + r_pages]

    def own_rows(cache, new):
        past = cache[own_pages].transpose(0, 2, 1, 3, 4).reshape(nb, N_HEADS, r_pages * PAGE_SIZE, HEAD_DIM)
        return jnp.concatenate([past, new.transpose(0, 2, 1, 3)], axis=2)

    k_own = own_rows(cache_k, k_new)
    v_own = own_rows(cache_v, v_new)
    own_mask = jnp.concatenate([jnp.ones((ds, r_pages * PAGE_SIZE), bool),
                                jnp.tril(jnp.ones((ds, ds), bool))], axis=1)
    if topk == 0:
        return moba_attend(q, k_own, v_own, own_mask).reshape(nb, ds, ATT_WIDTH)
    k_past = cache_k[page_table[:, :n_full * ppb]]
    k_mean = jnp.mean(k_past.astype(jnp.float32).reshape(nb, n_full, ppb, N_HEADS, PAGE_SIZE, HEAD_DIM), axis=(2, 4))
    scores = jnp.einsum('nqhd,nbhd->nqhb', q.astype(jnp.float32), k_mean)
    _, sel = lax.top_k(scores, topk)
    logical = sel[..., None] * ppb + jnp.arange(ppb)
    phys = page_table[jnp.arange(nb)[:, None, None, None, None], logical]
    h_idx = jnp.arange(N_HEADS)[None, None, :, None, None]
    k_sel = cache_k[phys, h_idx].reshape(nb, ds, N_HEADS, topk, MOBA_BLOCK, HEAD_DIM)
    v_sel = cache_v[phys, h_idx].reshape(nb, ds, N_HEADS, topk, MOBA_BLOCK, HEAD_DIM)
    out = moba_attend(q, k_own, v_own, own_mask, k_sel, v_sel)
    return out.reshape(nb, ds, ATT_WIDTH)


def s5_mixer(u, s0_re, s0_im, a_re, a_im, log_dt, b_re, b_im, c_re, c_im, d_skip, w_glu, b_glu):
    n, l, _ = u.shape
    uf = u.astype(jnp.float32).reshape(n, l, SSM_GROUPS, SSM_GROUP)
    lam = lax.complex(a_re.astype(jnp.float32), a_im.astype(jnp.float32))
    dt = jnp.exp(log_dt.astype(jnp.float32))[:, None]
    lam_bar = jnp.exp(lam * dt)
    b_mat = lax.complex(b_re.astype(jnp.float32), b_im.astype(jnp.float32))
    b_bar = ((lam_bar - 1.0) / lam)[..., None] * b_mat
    c_mat = lax.complex(c_re.astype(jnp.float32), c_im.astype(jnp.float32))
    bu = jnp.einsum('gph,nlgh->nlgp', b_bar, uf)
    s0 = lax.complex(s0_re.astype(jnp.float32), s0_im.astype(jnp.float32))
    bu = bu.at[:, 0].add(lam_bar * s0)
    a = jnp.broadcast_to(lam_bar, bu.shape)

    def combine(e1, e2):
        a1, b1 = e1
        a2, b2 = e2
        return (a1 * a2, a2 * b1 + b2)

    _, states = lax.associative_scan(combine, (a, bu), axis=1)
    y = jnp.real(jnp.einsum('ghp,nlgp->nlgh', c_mat, states))
    y = y + d_skip.astype(jnp.float32).reshape(SSM_GROUPS, SSM_GROUP) * uf
    y = jax.nn.gelu(y.reshape(n, l, SSM_WIDTH))
    y = y * jax.nn.sigmoid(y @ w_glu.astype(jnp.float32) + b_glu.astype(jnp.float32))
    s_last = states[:, -1]
    return y.astype(u.dtype), jnp.real(s_last), jnp.imag(s_last)


def to_pages(t):
    n, l = t.shape[:2]
    return t.reshape(n, l // PAGE_SIZE, PAGE_SIZE, N_HEADS, HEAD_DIM).transpose(0, 1, 3, 2, 4)


def setup_inputs(seed: int = 0) -> dict:
    key = jax.random.key(seed)
    keys = iter(jax.random.split(key, 48))
    n_pages = PAST_LEN // PAGE_SIZE
    n_used = DEC_BATCH * n_pages
    n_pool = n_used + max(1, n_used // 4)

    def nrm(shape, scale):
        return scale * jax.random.normal(next(keys), shape, jnp.float32)

    def gain(width):
        return 1.0 + nrm((DEPTH, width), 0.02)

    x_prompt = nrm((BATCH, SEQ, D_MODEL), 1.0)
    x_sample = nrm((DEC_BATCH, DEC_SEQ, D_MODEL), 1.0)
    cache_k = nrm((DEPTH, n_pool, N_HEADS, PAGE_SIZE, HEAD_DIM), 1.0)
    cache_v = nrm((DEPTH, n_pool, N_HEADS, PAGE_SIZE, HEAD_DIM), 1.0)
    state_ssm_re = nrm((DEPTH, DEC_BATCH, SSM_GROUPS, SSM_STATE), 0.3)
    state_ssm_im = nrm((DEPTH, DEC_BATCH, SSM_GROUPS, SSM_STATE), 0.3)
    page_table = jax.random.permutation(next(keys), n_pool)[:n_used].reshape(DEC_BATCH, n_pages).astype(jnp.int32)

    g_ffn1_pre = gain(D_MODEL)
    g_ffn1_post = gain(D_MODEL)
    w_ffn1_gate = nrm((DEPTH, D_MODEL, D_FF), D_MODEL ** -0.5)
    w_ffn1_up = nrm((DEPTH, D_MODEL, D_FF), D_MODEL ** -0.5)
    w_ffn1_down = nrm((DEPTH, D_FF, D_MODEL), D_FF ** -0.5)
    g_mix_pre = gain(D_MODEL)
    g_mix_post = gain(D_MODEL)
    w_in = nrm((DEPTH, D_MODEL, PROJ_WIDTH), D_MODEL ** -0.5)
    w_out = nrm((DEPTH, MIX_WIDTH, D_MODEL), MIX_WIDTH ** -0.5)
    ssm_a_re = -0.5 * jnp.exp(nrm((DEPTH, SSM_GROUPS, SSM_STATE), 0.05))
    ssm_a_im = math.pi * jnp.arange(SSM_STATE, dtype=jnp.float32)[None, None, :] + nrm((DEPTH, SSM_GROUPS, SSM_STATE), 0.01)
    ssm_log_dt = math.log(DT_MIN) + jax.random.uniform(next(keys), (DEPTH, SSM_GROUPS), jnp.float32) * (math.log(DT_MAX) - math.log(DT_MIN))
    ssm_b_re = nrm((DEPTH, SSM_GROUPS, SSM_STATE, SSM_GROUP), (2 * SSM_GROUP) ** -0.5)
    ssm_b_im = nrm((DEPTH, SSM_GROUPS, SSM_STATE, SSM_GROUP), (2 * SSM_GROUP) ** -0.5)
    ssm_c_re = nrm((DEPTH, SSM_GROUPS, SSM_GROUP, SSM_STATE), (2 * SSM_STATE) ** -0.5)
    ssm_c_im = nrm((DEPTH, SSM_GROUPS, SSM_GROUP, SSM_STATE), (2 * SSM_STATE) ** -0.5)
    ssm_d = nrm((DEPTH, SSM_WIDTH), 0.5)
    w_glu = nrm((DEPTH, SSM_WIDTH, SSM_WIDTH), SSM_WIDTH ** -0.5)
    b_glu = nrm((DEPTH, SSM_WIDTH), 0.01)
    g_ffn2_pre = gain(D_MODEL)
    g_ffn2_post = gain(D_MODEL)
    w_ffn2_gate = nrm((DEPTH, D_MODEL, D_FF), D_MODEL ** -0.5)
    w_ffn2_up = nrm((DEPTH, D_MODEL, D_FF), D_MODEL ** -0.5)
    w_ffn2_down = nrm((DEPTH, D_FF, D_MODEL), D_FF ** -0.5)
    return {
        'x_prompt': x_prompt, 'x_sample': x_sample,
        'cache_k': cache_k, 'cache_v': cache_v,
        'state_ssm_re': state_ssm_re, 'state_ssm_im': state_ssm_im,
        'page_table': page_table,
        'g_ffn1_pre': g_ffn1_pre, 'g_ffn1_post': g_ffn1_post,
        'w_ffn1_gate': w_ffn1_gate, 'w_ffn1_up': w_ffn1_up, 'w_ffn1_down': w_ffn1_down,
        'g_mix_pre': g_mix_pre, 'g_mix_post': g_mix_post,
        'w_in': w_in, 'w_out': w_out,
        'ssm_a_re': ssm_a_re, 'ssm_a_im': ssm_a_im, 'ssm_log_dt': ssm_log_dt,
        'ssm_b_re': ssm_b_re, 'ssm_b_im': ssm_b_im, 'ssm_c_re': ssm_c_re, 'ssm_c_im': ssm_c_im,
        'ssm_d': ssm_d, 'w_glu': w_glu, 'b_glu': b_glu,
        'g_ffn2_pre': g_ffn2_pre, 'g_ffn2_post': g_ffn2_post,
        'w_ffn2_gate': w_ffn2_gate, 'w_ffn2_up': w_ffn2_up, 'w_ffn2_down': w_ffn2_down,
    }


def reference(x_prompt, x_sample, cache_k, cache_v, state_ssm_re, state_ssm_im, page_table,
              g_ffn1_pre, g_ffn1_post, w_ffn1_gate, w_ffn1_up, w_ffn1_down,
              g_mix_pre, g_mix_post, w_in, w_out,
              ssm_a_re, ssm_a_im, ssm_log_dt, ssm_b_re, ssm_b_im, ssm_c_re, ssm_c_im,
              ssm_d, w_glu, b_glu,
              g_ffn2_pre, g_ffn2_post, w_ffn2_gate, w_ffn2_up, w_ffn2_down):
    n_p, l_p, _ = x_prompt.shape
    n_s, l_s, _ = x_sample.shape
    pos_p = jnp.arange(l_p, dtype=jnp.int32)
    pos_s = PAST_LEN + jnp.arange(l_s, dtype=jnp.int32)
    zero_state = jnp.zeros((n_p, SSM_GROUPS, SSM_STATE), jnp.float32)
    hp, hs = x_prompt, x_sample
    kp_new, vp_new, ks_new, vs_new = [], [], [], []
    re_p_new, im_p_new, re_s_new, im_s_new = [], [], [], []
    for li in range(DEPTH):
        ffn1 = (g_ffn1_pre[li], g_ffn1_post[li], w_ffn1_gate[li], w_ffn1_up[li], w_ffn1_down[li])
        ffn2 = (g_ffn2_pre[li], g_ffn2_post[li], w_ffn2_gate[li], w_ffn2_up[li], w_ffn2_down[li])
        ssm_par = (ssm_a_re[li], ssm_a_im[li], ssm_log_dt[li], ssm_b_re[li], ssm_b_im[li],
                   ssm_c_re[li], ssm_c_im[li], ssm_d[li], w_glu[li], b_glu[li])
        hp = half_step_ffn(hp, *ffn1)
        hs = half_step_ffn(hs, *ffn1)
        up = rms_norm(hp, g_mix_pre[li])
        us = rms_norm(hs, g_mix_pre[li])
        qp, kp, vp, sp = project(up, w_in[li], pos_p)
        qs, ks, vs, ss = project(us, w_in[li], pos_s)
        att_p = moba_prompt(qp, kp, vp)
        att_s = moba_sample(qs, ks, vs, cache_k[li], cache_v[li], page_table)
        ssm_p, r_p, i_p = s5_mixer(sp, zero_state, zero_state, *ssm_par)
        ssm_s, r_s, i_s = s5_mixer(ss, state_ssm_re[li], state_ssm_im[li], *ssm_par)
        hp = hp + rms_norm(jnp.concatenate([att_p, ssm_p], axis=-1) @ w_out[li], g_mix_post[li])
        hs = hs + rms_norm(jnp.concatenate([att_s, ssm_s], axis=-1) @ w_out[li], g_mix_post[li])
        hp = half_step_ffn(hp, *ffn2)
        hs = half_step_ffn(hs, *ffn2)
        kp_new.append(to_pages(kp))
        vp_new.append(to_pages(vp))
        ks_new.append(ks.transpose(0, 2, 1, 3))
        vs_new.append(vs.transpose(0, 2, 1, 3))
        re_p_new.append(r_p)
        im_p_new.append(i_p)
        re_s_new.append(r_s)
        im_s_new.append(i_s)
    return (hp, hs, jnp.stack(kp_new), jnp.stack(vp_new), jnp.stack(ks_new), jnp.stack(vs_new),
            jnp.stack(re_p_new), jnp.stack(im_p_new), jnp.stack(re_s_new), jnp.stack(im_s_new))
```

```python
import functools
import math

import jax
import jax.numpy as jnp
from jax import lax
from jax.experimental import pallas as pl
from jax.experimental.pallas import tpu as pltpu

HEAD_DIM = 64
N_HEADS = 8
ATT_WIDTH = N_HEADS * HEAD_DIM
SSM_GROUP = 16
SSM_GROUPS = 32
SSM_WIDTH = SSM_GROUP * SSM_GROUPS
SSM_STATE = 64
SSM_FLAT = SSM_GROUPS * SSM_STATE
MOBA_BLOCK = 256
MOBA_TOPK = 3
PAGE_SIZE = 128
PAGES_PER_BLOCK = MOBA_BLOCK // PAGE_SIZE
ROPE_THETA = 10000.0
RMS_EPS = 1e-6
LOG2E = math.log2(math.e)

LANES = 128
SUBLANES = 8
VMEM_LIMIT_BYTES = 56 * 1024 * 1024
HEAD_PAIR = LANES // HEAD_DIM
SSM_CHUNKS = SSM_WIDTH // LANES
SSM_CHUNK_STATE = SSM_FLAT // SSM_CHUNKS

F32 = jnp.float32
BF16 = jnp.bfloat16
NEG_INF = float("-inf")


def _const_spec(shape):
    zeros = (0,) * len(shape)
    return pl.BlockSpec(shape, lambda *_: zeros, pipeline_mode=pl.Buffered(1))


def _params(*semantics):
    return pltpu.CompilerParams(dimension_semantics=semantics, vmem_limit_bytes=VMEM_LIMIT_BYTES)


def _dot(a, b):
    return jnp.dot(a, b, preferred_element_type=F32)


def _dot_nt(a, b, precision=None):
    return lax.dot_general(a, b, (((1,), (1,)), ((), ())), precision=precision,
                           preferred_element_type=F32)


def _rms(x, g):
    return x * lax.rsqrt(jnp.mean(x * x, axis=-1, keepdims=True) + RMS_EPS) * g


def _half_step_ffn(x, g_pre, g_post, wg_ref, wu_ref, wd_ref):
    h = _rms(x, g_pre).astype(BF16)
    gate = _dot(h, wg_ref[...])
    up = _dot(h, wu_ref[...])
    act = (gate * jax.nn.sigmoid(gate) * up).astype(BF16)
    return x + 0.5 * _rms(_dot(act, wd_ref[...]), g_post)


def _rope(x, cos, sin_signed):
    first_half = (lax.broadcasted_iota(jnp.int32, cos.shape, 1) % HEAD_DIM) < HEAD_DIM // 2
    out = []
    for c in range(x.shape[1] // LANES):
        xc = x[:, c * LANES:(c + 1) * LANES]
        partner = jnp.where(first_half, pltpu.roll(xc, LANES - HEAD_DIM // 2, 1),
                            pltpu.roll(xc, HEAD_DIM // 2, 1))
        out.append(xc * cos + partner * sin_signed)
    return jnp.concatenate(out, axis=1)


def _store_pages(ref, val):
    n_pages = val.shape[0] // PAGE_SIZE
    for h in range(N_HEADS):
        ref[0, :, h, :, :] = val[:, h * HEAD_DIM:(h + 1) * HEAD_DIM].reshape(n_pages, PAGE_SIZE, HEAD_DIM)


def _ffn_proj_kernel(x_ref, cos_ref, sin_ref, g_pre_ref, g_post_ref, g_mix_ref,
                     wg_ref, wu_ref, wd_ref, win_ref,
                     h_ref, q_ref, k_ref, v_ref, s_ref):
    h1 = _half_step_ffn(x_ref[0], g_pre_ref[...], g_post_ref[...], wg_ref, wu_ref, wd_ref)
    h_ref[0] = h1
    u = _rms(h1, g_mix_ref[...]).astype(BF16)
    z = _dot(u, win_ref[...])
    cos, sin = cos_ref[...], sin_ref[...]
    q_ref[0] = _rope(z[:, :ATT_WIDTH], cos, sin)
    _store_pages(k_ref, _rope(z[:, ATT_WIDTH:2 * ATT_WIDTH], cos, sin))
    _store_pages(v_ref, z[:, 2 * ATT_WIDTH:3 * ATT_WIDTH])
    s_ref[0] = z[:, 3 * ATT_WIDTH:]


def _ffn_proj(x, cos, sin, g_pre, g_post, g_mix, wg, wu, wd, win, tm):
    n, l, d = x.shape
    f = wg.shape[1]
    row = lambda w: pl.BlockSpec((1, tm, w), lambda b, j: (b, j, 0))
    tab = pl.BlockSpec((tm, LANES), lambda b, j: (j, 0))
    pages = pl.BlockSpec((1, tm // PAGE_SIZE, N_HEADS, PAGE_SIZE, HEAD_DIM), lambda b, j: (b, j, 0, 0, 0))
    page_shape = jax.ShapeDtypeStruct((n, l // PAGE_SIZE, N_HEADS, PAGE_SIZE, HEAD_DIM), F32)
    return pl.pallas_call(
        _ffn_proj_kernel,
        grid=(n, l // tm),
        in_specs=[row(d), tab, tab, _const_spec((1, d)), _const_spec((1, d)), _const_spec((1, d)),
                  _const_spec((d, f)), _const_spec((d, f)), _const_spec((f, d)),
                  _const_spec(win.shape)],
        out_specs=[row(d), row(ATT_WIDTH), pages, pages, row(SSM_WIDTH)],
        out_shape=[jax.ShapeDtypeStruct((n, l, d), F32), jax.ShapeDtypeStruct((n, l, ATT_WIDTH), F32),
                   page_shape, page_shape, jax.ShapeDtypeStruct((n, l, SSM_WIDTH), F32)],
        compiler_params=_params("parallel", "parallel"),
        name="ffn_proj",
    )(x, cos, sin, g_pre, g_post, g_mix, wg, wu, wd, win)


def _mix_ffn_kernel(h_ref, att_ref, ssm_ref, wo_ref, g_mix_ref, g_pre_ref, g_post_ref,
                    wg_ref, wu_ref, wd_ref, y_ref):
    mix = _dot(att_ref[0], wo_ref[0]) + _dot(ssm_ref[0], wo_ref[1])
    h2 = h_ref[0] + _rms(mix, g_mix_ref[...])
    y_ref[0] = _half_step_ffn(h2, g_pre_ref[...], g_post_ref[...], wg_ref, wu_ref, wd_ref)


def _mix_ffn(h, att, ssm, wo, g_mix, g_pre, g_post, wg, wu, wd, tm):
    n, l, d = h.shape
    f = wg.shape[1]
    row = lambda w: pl.BlockSpec((1, tm, w), lambda b, j: (b, j, 0))
    return pl.pallas_call(
        _mix_ffn_kernel,
        grid=(n, l // tm),
        in_specs=[row(d), row(ATT_WIDTH), row(SSM_WIDTH), _const_spec(wo.shape),
                  _const_spec((1, d)), _const_spec((1, d)), _const_spec((1, d)),
                  _const_spec((d, f)), _const_spec((d, f)), _const_spec((f, d))],
        out_specs=row(d),
        out_shape=jax.ShapeDtypeStruct((n, l, d), F32),
        compiler_params=_params("parallel", "parallel"),
        name="mix_ffn",
    )(h, att, ssm, wo, g_mix, g_pre, g_post, wg, wu, wd)


def _select_topk(gate, n_valid, topk):
    rows = lax.broadcasted_iota(jnp.int32, gate.shape, 0)
    valid = rows < n_valid
    g = jnp.where(valid, gate, NEG_INF)
    sel = jnp.zeros(gate.shape, jnp.bool_)
    for _ in range(topk):
        best = jnp.max(g, axis=0, keepdims=True)
        idx = jnp.min(jnp.where(g == best, rows, gate.shape[0]), axis=0, keepdims=True)
        pick = rows == idx
        sel = jnp.logical_or(sel, pick)
        g = jnp.where(pick, NEG_INF, g)
    return jnp.logical_and(sel, valid)


def _moba_prompt_kernel(topk, q_ref, k_ref, v_ref, o_ref, k_bf, v_t, k_mean, bias):
    n_blk = k_bf.shape[1] // MOBA_BLOCK
    i = pl.program_id(2)

    @pl.when(i == 0)
    def _():
        for hh in range(HEAD_PAIR):
            def prep(j, carry):
                pg = pl.multiple_of(j * PAGES_PER_BLOCK, PAGES_PER_BLOCK)
                row = pl.multiple_of(j * MOBA_BLOCK, MOBA_BLOCK)
                kb = k_ref[0, pl.ds(pg, PAGES_PER_BLOCK), hh, :, :].reshape(MOBA_BLOCK, HEAD_DIM)
                vb = v_ref[0, pl.ds(pg, PAGES_PER_BLOCK), hh, :, :].reshape(MOBA_BLOCK, HEAD_DIM)
                k_bf[hh, pl.ds(row, MOBA_BLOCK), :] = kb.astype(BF16)
                v_t[hh, :, pl.ds(row, MOBA_BLOCK)] = vb.T.astype(BF16)
                k_mean[hh, pl.ds(j, 1), :] = jnp.mean(kb, axis=0, keepdims=True)
                return carry
            lax.fori_loop(0, n_blk, prep, 0)

    key_idx = lax.broadcasted_iota(jnp.int32, (MOBA_BLOCK, MOBA_BLOCK), 0)
    qry_idx = lax.broadcasted_iota(jnp.int32, (MOBA_BLOCK, MOBA_BLOCK), 1)
    causal = key_idx <= qry_idx
    own = pl.multiple_of(i * MOBA_BLOCK, MOBA_BLOCK)
    outs = []
    for hh in range(HEAD_PAIR):
        qh = q_ref[0][:, hh * HEAD_DIM:(hh + 1) * HEAD_DIM]
        if topk > 0:
            gate = _dot_nt(k_mean[hh], qh, precision=lax.Precision.HIGHEST)
            bias[hh] = jnp.where(_select_topk(gate, i, topk), 0.0, NEG_INF)
        qs = (qh * (HEAD_DIM ** -0.5 * LOG2E)).astype(BF16)
        s = jnp.where(causal, _dot_nt(k_bf[hh, pl.ds(own, MOBA_BLOCK), :], qs), NEG_INF)
        m = jnp.max(s, axis=0, keepdims=True)
        p = jnp.exp2(s - m)
        l = jnp.sum(p, axis=0, keepdims=True)
        acc = _dot(v_t[hh, :, pl.ds(own, MOBA_BLOCK)], p.astype(BF16))

        def past_block(j, carry, hh=hh, qs=qs):
            m, l, acc = carry
            row = pl.multiple_of(j * MOBA_BLOCK, MOBA_BLOCK)
            s = _dot_nt(k_bf[hh, pl.ds(row, MOBA_BLOCK), :], qs) + bias[hh, pl.ds(j, 1), :]
            m_new = jnp.maximum(m, jnp.max(s, axis=0, keepdims=True))
            alpha = jnp.exp2(m - m_new)
            p = jnp.exp2(s - m_new)
            l = alpha * l + jnp.sum(p, axis=0, keepdims=True)
            acc = alpha * acc + _dot(v_t[hh, :, pl.ds(row, MOBA_BLOCK)], p.astype(BF16))
            return m_new, l, acc

        if topk > 0:
            m, l, acc = lax.fori_loop(0, i, past_block, (m, l, acc))
        outs.append((acc / l).T)
    o_ref[0] = jnp.concatenate(outs, axis=1).astype(o_ref.dtype)


def _moba_prompt(q, k_pages, v_pages):
    n, l, _ = q.shape
    n_blk = l // MOBA_BLOCK
    topk = min(MOBA_TOPK, (l - 1) // MOBA_BLOCK)
    kv = pl.BlockSpec((1, l // PAGE_SIZE, HEAD_PAIR, PAGE_SIZE, HEAD_DIM),
                      lambda b, hp, i: (b, 0, hp, 0, 0), pipeline_mode=pl.Buffered(1))
    qo = pl.BlockSpec((1, MOBA_BLOCK, LANES), lambda b, hp, i: (b, i, hp))
    return pl.pallas_call(
        functools.partial(_moba_prompt_kernel, topk),
        grid=(n, N_HEADS // HEAD_PAIR, n_blk),
        in_specs=[qo, kv, kv],
        out_specs=qo,
        out_shape=jax.ShapeDtypeStruct((n, l, ATT_WIDTH), BF16),
        scratch_shapes=[pltpu.VMEM((HEAD_PAIR, l, HEAD_DIM), BF16),
                        pltpu.VMEM((HEAD_PAIR, HEAD_DIM, l), BF16),
                        pltpu.VMEM((HEAD_PAIR, n_blk, HEAD_DIM), F32),
                        pltpu.VMEM((HEAD_PAIR, n_blk, MOBA_BLOCK), F32)],
        compiler_params=_params("parallel", "parallel", "arbitrary"),
        name="moba_prompt",
    )(q, k_pages, v_pages)


def _s5_params(a_re, a_im, log_dt, b_re, b_im, c_re, c_im):
    dt = jnp.exp(log_dt.astype(F32))[:, None]
    a_re, a_im = a_re.astype(F32), a_im.astype(F32)
    mag = jnp.exp(a_re * dt)
    lam_re, lam_im = mag * jnp.cos(a_im * dt), mag * jnp.sin(a_im * dt)
    x, y = lam_re - 1.0, lam_im
    den = a_re * a_re + a_im * a_im
    r_re, r_im = (x * a_re + y * a_im) / den, (y * a_re - x * a_im) / den
    b_re, b_im = b_re.astype(F32), b_im.astype(F32)
    bb_re = r_re[..., None] * b_re - r_im[..., None] * b_im
    bb_im = r_re[..., None] * b_im + r_im[..., None] * b_re
    gpc = SSM_GROUPS // SSM_CHUNKS
    eye = jnp.eye(gpc, dtype=F32)

    def in_map(w):
        w = w.reshape(SSM_CHUNKS, gpc, SSM_STATE, SSM_GROUP)
        return jnp.einsum('cgph,gk->cghkp', w, eye).reshape(SSM_CHUNKS, LANES, SSM_CHUNK_STATE).astype(BF16)

    def out_map(w):
        w = w.astype(F32).reshape(SSM_CHUNKS, gpc, SSM_GROUP, SSM_STATE)
        return jnp.einsum('cghp,gk->cgpkh', w, eye).reshape(SSM_CHUNKS, SSM_CHUNK_STATE, LANES).astype(BF16)

    return (lam_re.reshape(1, SSM_FLAT), lam_im.reshape(1, SSM_FLAT),
            in_map(bb_re), in_map(bb_im), out_map(c_re), out_map(-c_im.astype(F32)))


def _s5_input(u, bre_ref, bim_ref, c):
    ub = u[:, c * LANES:(c + 1) * LANES].astype(BF16)
    return _dot(ub, bre_ref[c]), _dot(ub, bim_ref[c])


def _s5_readout(s_re, s_im, cre_ref, cim_ref, c):
    return _dot(s_re.astype(BF16), cre_ref[c]) + _dot(s_im.astype(BF16), cim_ref[c])


def _s5_gate(y, u, d_ref, wglu_ref, bglu_ref):
    y = jax.nn.gelu(y + d_ref[...] * u)
    return y * jax.nn.sigmoid(_dot(y.astype(BF16), wglu_ref[...]) + bglu_ref[...])


def _s5_scan_kernel(u_ref, s0_ref, ar_ref, ai_ref, bre_ref, bim_ref, cre_ref, cim_ref,
                    d_ref, wglu_ref, bglu_ref, y_ref, sfin_ref, u_bt, u_tb, bu, y_tb, y_bt, state):
    n, t_len, _ = u_ref.shape
    rows = n * t_len

    @pl.when(pl.program_id(0) == 0)
    def _():
        state[...] = s0_ref[...]

    for c in range(SSM_CHUNKS):
        u_bt[c] = u_ref[:, :, c * LANES:(c + 1) * LANES].reshape(rows, LANES)

    def to_time_major(t, carry):
        r = pl.ds(pl.multiple_of(t * n, n), n)
        for c in range(SSM_CHUNKS):
            u_tb[r, c * LANES:(c + 1) * LANES] = u_bt[c, pl.ds(t, n, stride=t_len), :]
        return carry
    lax.fori_loop(0, t_len, to_time_major, 0)

    u = u_tb[...]
    for c in range(SSM_CHUNKS):
        lo, hi = c * SSM_CHUNK_STATE, (c + 1) * SSM_CHUNK_STATE
        bu[:, lo:hi], bu[:, SSM_FLAT + lo:SSM_FLAT + hi] = _s5_input(u, bre_ref, bim_ref, c)

    ys = []
    for c in range(SSM_CHUNKS):
        lo, hi = c * SSM_CHUNK_STATE, (c + 1) * SSM_CHUNK_STATE
        a_re = jnp.broadcast_to(ar_ref[:, lo:hi], (n, SSM_CHUNK_STATE))
        a_im = jnp.broadcast_to(ai_ref[:, lo:hi], (n, SSM_CHUNK_STATE))

        def step(t, carry, lo=lo, hi=hi, a_re=a_re, a_im=a_im):
            s_re, s_im = carry
            r = pl.ds(pl.multiple_of(t * n, n), n)
            n_re = a_re * s_re - a_im * s_im + bu[r, lo:hi]
            n_im = a_re * s_im + a_im * s_re + bu[r, SSM_FLAT + lo:SSM_FLAT + hi]
            bu[r, lo:hi] = n_re
            bu[r, SSM_FLAT + lo:SSM_FLAT + hi] = n_im
            return n_re, n_im

        s_re, s_im = lax.fori_loop(0, t_len, step, (state[:, lo:hi], state[:, SSM_FLAT + lo:SSM_FLAT + hi]),
                                   unroll=8)
        state[:, lo:hi] = s_re
        state[:, SSM_FLAT + lo:SSM_FLAT + hi] = s_im
        ys.append(_s5_readout(bu[:, lo:hi], bu[:, SSM_FLAT + lo:SSM_FLAT + hi], cre_ref, cim_ref, c))

    y_tb[...] = _s5_gate(jnp.concatenate(ys, axis=1), u, d_ref, wglu_ref, bglu_ref)

    def to_batch_major(t, carry):
        r = pl.ds(pl.multiple_of(t * n, n), n)
        for c in range(SSM_CHUNKS):
            y_bt[c, pl.ds(t, n, stride=t_len), :] = y_tb[r, c * LANES:(c + 1) * LANES]
        return carry
    lax.fori_loop(0, t_len, to_batch_major, 0)
    for c in range(SSM_CHUNKS):
        y_ref[:, :, c * LANES:(c + 1) * LANES] = y_bt[c].reshape(n, t_len, LANES).astype(y_ref.dtype)
    sfin_ref[...] = state[...]


def _s5_scan(u, s0, s5p, d_skip, wglu, bglu, t_len):
    n, l, _ = u.shape
    assert n == SUBLANES, "the scan keeps one sequence per sublane"
    rows = n * t_len
    consts = [_const_spec(p.shape) for p in s5p]
    return pl.pallas_call(
        _s5_scan_kernel,
        grid=(l // t_len,),
        in_specs=[pl.BlockSpec((n, t_len, SSM_WIDTH), lambda t: (0, t, 0)), _const_spec(s0.shape), *consts,
                  _const_spec(d_skip.shape), _const_spec(wglu.shape), _const_spec(bglu.shape)],
        out_specs=[pl.BlockSpec((n, t_len, SSM_WIDTH), lambda t: (0, t, 0)),
                   pl.BlockSpec(s0.shape, lambda t: (0, 0))],
        out_shape=[jax.ShapeDtypeStruct((n, l, SSM_WIDTH), BF16), jax.ShapeDtypeStruct(s0.shape, F32)],
        scratch_shapes=[pltpu.VMEM((SSM_CHUNKS, rows, LANES), F32), pltpu.VMEM((rows, SSM_WIDTH), F32),
                        pltpu.VMEM((rows, 2 * SSM_FLAT), F32), pltpu.VMEM((rows, SSM_WIDTH), F32),
                        pltpu.VMEM((SSM_CHUNKS, rows, LANES), F32), pltpu.VMEM(s0.shape, F32)],
        compiler_params=_params("arbitrary"),
        name="s5_scan",
    )(u, s0, *s5p, d_skip, wglu, bglu)


def _s5_step_kernel(u_ref, s0_ref, ar_ref, ai_ref, bre_ref, bim_ref, cre_ref, cim_ref,
                    d_ref, wglu_ref, bglu_ref, y_ref, s_ref):
    u = u_ref[...]
    ys = []
    for c in range(SSM_CHUNKS):
        lo, hi = c * SSM_CHUNK_STATE, (c + 1) * SSM_CHUNK_STATE
        bu_re, bu_im = _s5_input(u, bre_ref, bim_ref, c)
        a_re, a_im = ar_ref[:, lo:hi], ai_ref[:, lo:hi]
        s_re, s_im = s0_ref[:, lo:hi], s0_ref[:, SSM_FLAT + lo:SSM_FLAT + hi]
        n_re = a_re * s_re - a_im * s_im + bu_re
        n_im = a_re * s_im + a_im * s_re + bu_im
        s_ref[:, lo:hi] = n_re
        s_ref[:, SSM_FLAT + lo:SSM_FLAT + hi] = n_im
        ys.append(_s5_readout(n_re, n_im, cre_ref, cim_ref, c))
    y_ref[...] = _s5_gate(jnp.concatenate(ys, axis=1), u, d_ref, wglu_ref, bglu_ref).astype(y_ref.dtype)


def _s5_step(u, s0, s5p, d_skip, wglu, bglu):
    return pl.pallas_call(
        _s5_step_kernel,
        out_shape=[jax.ShapeDtypeStruct(u.shape, BF16), jax.ShapeDtypeStruct(s0.shape, F32)],
        compiler_params=pltpu.CompilerParams(vmem_limit_bytes=VMEM_LIMIT_BYTES),
        name="s5_step",
    )(u, s0, *s5p, d_skip, wglu, bglu)


SAMPLE_CHUNK_PAGES = 8


def _moba_select_kernel(n_pages, pt_ref, q_ref, cache_ref, sel_ref, buf, sem, k_mean):
    b = pl.program_id(0)
    n_chunks = n_pages // SAMPLE_CHUNK_PAGES

    def copies(seq, chunk, slot):
        return [pltpu.make_async_copy(
            cache_ref.at[pt_ref[seq * n_pages + chunk * SAMPLE_CHUNK_PAGES + p]],
            buf.at[slot, p], sem.at[slot]) for p in range(SAMPLE_CHUNK_PAGES)]

    def start(seq, chunk, slot):
        for cp in copies(seq, chunk, slot):
            cp.start()

    @pl.when(b == 0)
    def _():
        start(b, 0, 0)

    for chunk in range(n_chunks):
        slot = chunk % 2
        if chunk + 1 < n_chunks:
            start(b, chunk + 1, 1 - slot)
        else:
            @pl.when(b + 1 < pl.num_programs(0))
            def _():
                start(b + 1, 0, (chunk + 1) % 2)
        for cp in copies(b, chunk, slot):
            cp.wait()
        for blk in range(SAMPLE_CHUNK_PAGES // PAGES_PER_BLOCK):
            pages = buf[slot, blk * PAGES_PER_BLOCK]
            for p in range(1, PAGES_PER_BLOCK):
                pages = pages + buf[slot, blk * PAGES_PER_BLOCK + p]
            k_mean[chunk * (SAMPLE_CHUNK_PAGES // PAGES_PER_BLOCK) + blk] = (
                jnp.sum(pages, axis=1) * (1.0 / MOBA_BLOCK))

    n_blk = k_mean.shape[0]
    cols = n_blk * N_HEADS
    sc = _dot_nt(q_ref[0], k_mean[...].reshape(cols, HEAD_DIM), precision=lax.Precision.HIGHEST)
    col = lax.broadcasted_iota(jnp.int32, sc.shape, 1)
    head = lax.broadcasted_iota(jnp.int32, sc.shape, 0)
    g = jnp.where(col % N_HEADS == head, sc, NEG_INF)
    lane = lax.broadcasted_iota(jnp.int32, sel_ref.shape[1:], 1)
    out = jnp.zeros(sel_ref.shape[1:], jnp.int32)
    for r in range(min(MOBA_TOPK, n_blk)):
        best = jnp.max(g, axis=1, keepdims=True)
        idx = jnp.min(jnp.where(g == best, col, cols), axis=1, keepdims=True)
        out = jnp.where(lane == r, idx // N_HEADS, out)
        g = jnp.where(col == idx, NEG_INF, g)
    sel_ref[0] = out


def _moba_select(q, cache_k, page_table):
    db, n_pages = page_table.shape
    assert n_pages % (2 * SAMPLE_CHUNK_PAGES) == 0
    n_blk = n_pages // PAGES_PER_BLOCK
    grid_spec = pltpu.PrefetchScalarGridSpec(
        num_scalar_prefetch=1,
        grid=(db,),
        in_specs=[pl.BlockSpec((1, N_HEADS, HEAD_DIM), lambda b, pt: (b, 0, 0)),
                  pl.BlockSpec(memory_space=pl.ANY)],
        out_specs=pl.BlockSpec((1, N_HEADS, LANES), lambda b, pt: (b, 0, 0)),
        scratch_shapes=[pltpu.VMEM((2, SAMPLE_CHUNK_PAGES, N_HEADS, PAGE_SIZE, HEAD_DIM), F32),
                        pltpu.SemaphoreType.DMA((2,)),
                        pltpu.VMEM((n_blk, N_HEADS, HEAD_DIM), F32)],
    )
    return pl.pallas_call(
        functools.partial(_moba_select_kernel, n_pages),
        grid_spec=grid_spec,
        out_shape=jax.ShapeDtypeStruct((db, N_HEADS, LANES), jnp.int32),
        compiler_params=_params("arbitrary"),
        name="moba_select",
    )(page_table.reshape(-1), q, cache_k)


def _moba_sample_kernel(topk, pg_ref, q_ref, kn_ref, vn_ref, ck_ref, cv_ref, o_ref, k_buf, v_buf, sem):
    b = pl.program_id(0)
    per_seq = N_HEADS * topk * PAGES_PER_BLOCK

    def copies(seq, slot):
        out = []
        for h in range(N_HEADS):
            for j in range(topk * PAGES_PER_BLOCK):
                pg = pg_ref[seq * per_seq + h * topk * PAGES_PER_BLOCK + j]
                dst = pl.ds(j * PAGE_SIZE, PAGE_SIZE)
                out.append(pltpu.make_async_copy(ck_ref.at[pg, h], k_buf.at[slot, h, dst], sem.at[slot]))
                out.append(pltpu.make_async_copy(cv_ref.at[pg, h], v_buf.at[slot, h, dst], sem.at[slot]))
        return out

    @pl.when(b == 0)
    def _():
        for cp in copies(b, 0):
            cp.start()

    slot = b % 2

    @pl.when(b + 1 < pl.num_programs(0))
    def _():
        for cp in copies(b + 1, 1 - slot):
            cp.start()

    for cp in copies(b, slot):
        cp.wait()

    q = q_ref[0] * (HEAD_DIM ** -0.5 * LOG2E)
    qb = q.astype(BF16)
    n_sel = k_buf.shape[2]
    head = lax.broadcasted_iota(jnp.int32, (N_HEADS, n_sel), 0)
    s = jnp.zeros((N_HEADS, n_sel), F32)
    for h in range(N_HEADS):
        s = jnp.where(head == h, _dot_nt(qb, k_buf[slot, h].astype(BF16)), s)
    s_own = jnp.sum(q * kn_ref[0], axis=1, keepdims=True)
    m = jnp.maximum(jnp.max(s, axis=1, keepdims=True), s_own)
    p = jnp.exp2(s - m)
    p_own = jnp.exp2(s_own - m)
    den = jnp.sum(p, axis=1, keepdims=True) + p_own
    pb = p.astype(BF16)
    head_o = lax.broadcasted_iota(jnp.int32, (N_HEADS, HEAD_DIM), 0)
    o = jnp.zeros((N_HEADS, HEAD_DIM), F32)
    for h in range(N_HEADS):
        o = jnp.where(head_o == h, _dot(pb, v_buf[slot, h].astype(BF16)), o)
    o_ref[0] = (o + p_own * vn_ref[0]) / den


def _moba_sample(q, k_new, v_new, cache_k, cache_v, pages):
    db = q.shape[0]
    topk = pages.shape[2]
    n_sel = topk * MOBA_BLOCK
    row = pl.BlockSpec((1, N_HEADS, HEAD_DIM), lambda b, pg: (b, 0, 0))
    grid_spec = pltpu.PrefetchScalarGridSpec(
        num_scalar_prefetch=1,
        grid=(db,),
        in_specs=[row, row, row, pl.BlockSpec(memory_space=pl.ANY), pl.BlockSpec(memory_space=pl.ANY)],
        out_specs=row,
        scratch_shapes=[pltpu.VMEM((2, N_HEADS, n_sel, HEAD_DIM), F32),
                        pltpu.VMEM((2, N_HEADS, n_sel, HEAD_DIM), F32),
                        pltpu.SemaphoreType.DMA((2,))],
    )
    return pl.pallas_call(
        functools.partial(_moba_sample_kernel, topk),
        grid_spec=grid_spec,
        out_shape=jax.ShapeDtypeStruct((db, N_HEADS, HEAD_DIM), F32),
        compiler_params=_params("arbitrary"),
        name="moba_sample",
    )(pages.reshape(-1), q, k_new, v_new, cache_k, cache_v)


def _rope_tables(pos):
    half = HEAD_DIM // 2
    inv_freq = ROPE_THETA ** (-jnp.arange(half, dtype=F32) / half)
    ang = pos.astype(F32)[:, None] * inv_freq[None, :]
    cos, sin = jnp.cos(ang), jnp.sin(ang)
    reps = LANES // HEAD_DIM
    return jnp.tile(cos, (1, 2 * reps)), jnp.tile(jnp.concatenate([-sin, sin], axis=1), (1, reps))


PROMPT_ROWS = 256
SCAN_STEPS = 64


def kernel(x_prompt, x_sample, cache_k, cache_v, state_ssm_re, state_ssm_im, page_table, g_ffn1_pre, g_ffn1_post, w_ffn1_gate, w_ffn1_up, w_ffn1_down, g_mix_pre, g_mix_post, w_in, w_out, ssm_a_re, ssm_a_im, ssm_log_dt, ssm_b_re, ssm_b_im, ssm_c_re, ssm_c_im, ssm_d, w_glu, b_glu, g_ffn2_pre, g_ffn2_post, w_ffn2_gate, w_ffn2_up, w_ffn2_down):
    n_p, l_p, d = x_prompt.shape
    n_s, l_s, _ = x_sample.shape
    assert l_s == 1, "the sample path handles one new token per sequence"
    depth = w_in.shape[0]
    past_len = page_table.shape[1] * PAGE_SIZE
    assert past_len % MOBA_BLOCK == 0, "the sample path assumes no partially filled own block"
    n_full = past_len // MOBA_BLOCK
    topk_s = min(MOBA_TOPK, n_full)

    cos_p, sin_p = _rope_tables(jnp.arange(l_p, dtype=jnp.int32))
    cos_s, sin_s = _rope_tables(jnp.full((n_s,), past_len, jnp.int32))
    zero_state = jnp.zeros((n_p, 2 * SSM_FLAT), F32)

    hp, hs = x_prompt, x_sample.reshape(1, n_s, d)
    outs = [[] for _ in range(8)]
    for li in range(depth):
        row = lambda g: g[li].reshape(1, -1).astype(F32)
        bf = lambda w: w[li].astype(BF16)
        ffn1 = (bf(w_ffn1_gate), bf(w_ffn1_up), bf(w_ffn1_down))
        ffn2 = (bf(w_ffn2_gate), bf(w_ffn2_up), bf(w_ffn2_down))
        win = bf(w_in)
        wo = bf(w_out).reshape(2, ATT_WIDTH, d)
        s5p = _s5_params(ssm_a_re[li], ssm_a_im[li], ssm_log_dt[li], ssm_b_re[li], ssm_b_im[li],
                         ssm_c_re[li], ssm_c_im[li])
        gate_args = (row(ssm_d), bf(w_glu), row(b_glu))

        h1, q, kp, vp, sp = _ffn_proj(hp, cos_p, sin_p, row(g_ffn1_pre), row(g_ffn1_post), row(g_mix_pre),
                                      *ffn1, win, PROMPT_ROWS)
        att = _moba_prompt(q, kp, vp)
        ssm, s_fin = _s5_scan(sp, zero_state, s5p, *gate_args, SCAN_STEPS)
        hp = _mix_ffn(h1, att, ssm, wo, row(g_mix_post), row(g_ffn2_pre), row(g_ffn2_post), *ffn2, PROMPT_ROWS)
        outs[0].append(kp)
        outs[1].append(vp)
        outs[4].append(s_fin[:, :SSM_FLAT].reshape(n_p, SSM_GROUPS, SSM_STATE))
        outs[5].append(s_fin[:, SSM_FLAT:].reshape(n_p, SSM_GROUPS, SSM_STATE))

        h1, q, ks, vs, ss = _ffn_proj(hs, cos_s, sin_s, row(g_ffn1_pre), row(g_ffn1_post), row(g_mix_pre),
                                      *ffn1, win, n_s)
        unpage = lambda t: t[0].transpose(0, 2, 1, 3).reshape(n_s, N_HEADS, HEAD_DIM)
        k_new, v_new = unpage(ks), unpage(vs)
        q = q.reshape(n_s, N_HEADS, HEAD_DIM)
        sel = _moba_select(q, cache_k[li], page_table)[:, :, :topk_s]
        logical = sel[..., None] * PAGES_PER_BLOCK + jnp.arange(PAGES_PER_BLOCK, dtype=jnp.int32)
        pages = page_table[jnp.arange(n_s)[:, None, None, None], logical]
        att = _moba_sample(q, k_new, v_new, cache_k[li], cache_v[li], pages)
        s0 = jnp.concatenate([state_ssm_re[li].reshape(n_s, SSM_FLAT), state_ssm_im[li].reshape(n_s, SSM_FLAT)],
                             axis=1).astype(F32)
        ssm, s_new = _s5_step(ss.reshape(n_s, SSM_WIDTH), s0, s5p, *gate_args)
        hs = _mix_ffn(h1, att.reshape(1, n_s, ATT_WIDTH).astype(BF16), ssm.reshape(1, n_s, SSM_WIDTH), wo,
                      row(g_mix_post), row(g_ffn2_pre), row(g_ffn2_post), *ffn2, n_s)
        outs[2].append(k_new.reshape(n_s, N_HEADS, 1, HEAD_DIM))
        outs[3].append(v_new.reshape(n_s, N_HEADS, 1, HEAD_DIM))
        outs[6].append(s_new[:, :SSM_FLAT].reshape(n_s, SSM_GROUPS, SSM_STATE))
        outs[7].append(s_new[:, SSM_FLAT:].reshape(n_s, SSM_GROUPS, SSM_STATE))

    return (hp, hs.reshape(n_s, 1, d), *[jnp.stack(o) for o in outs])
```
